```python
import jax, jax.numpy as jnp
from jax import lax
import numpy as np

D_MODEL = 1024
BATCH = 8
SEQ = 4096
DEPTH = 4

MEM_LEN = 256
HEAD_DIM = 64
EPS = 1e-6
GM_WIDTH = D_MODEL // 4
GM_GROUPS = GM_WIDTH // HEAD_DIM
CHUNK = 128
POOL_WIDTH = D_MODEL // 4
POOL_WINDOWS = (2, 4, 8, 16)
POOL_GROUPS = len(POOL_WINDOWS)
POOL_GROUP_DIM = POOL_WIDTH // POOL_GROUPS
ATT_WIDTH = D_MODEL // 2
ATT_Q_HEADS = ATT_WIDTH // HEAD_DIM
ATT_KV_HEADS = ATT_Q_HEADS // 4
ATT_GROUP = ATT_Q_HEADS // ATT_KV_HEADS
WINDOW = 128
ROPE_THETA = 10000.0
D_MIX = GM_WIDTH + POOL_WIDTH + ATT_WIDTH
IN_SIZES = (2 * GM_WIDTH, POOL_WIDTH, ATT_Q_HEADS * HEAD_DIM, ATT_KV_HEADS * HEAD_DIM, ATT_KV_HEADS * HEAD_DIM)
IN_SPLITS = tuple(int(s) for s in np.cumsum(IN_SIZES)[:-1])
D_IN = sum(IN_SIZES)
X_HEADS = 4
X_HEAD_DIM = D_MODEL // X_HEADS
D_FF = -(-8 * D_MODEL // (3 * 256)) * 256

kernel_name = "hybrid_gmlp_pool_swa_sink_trunk"


def rms_norm(x, g):
    xf = x.astype(jnp.float32)
    y = xf * lax.rsqrt(jnp.mean(xf * xf, axis=-1, keepdims=True) + EPS)
    return (y * g.astype(jnp.float32)).astype(x.dtype)


def spatial_gating(uv, v_gain, w_s, b_s):
    B, S, _ = uv.shape
    u, v = jnp.split(uv, 2, axis=-1)
    v = v.reshape(B, S // CHUNK, CHUNK, GM_GROUPS, HEAD_DIM)
    v = rms_norm(v, v_gain.reshape(GM_GROUPS, HEAD_DIM))
    causal = jnp.tril(jnp.ones((CHUNK, CHUNK), dtype=bool))
    w = jnp.where(causal[None], w_s, jnp.zeros_like(w_s))
    mixed = jnp.einsum('gts,bcsgd->bctgd', w, v) + b_s.T[None, None, :, :, None]
    return u * mixed.reshape(B, S, GM_WIDTH)


def multiscale_pool(p, pool_w, pool_scale):
    B, S, _ = p.shape
    pf = p.astype(jnp.float32)
    cs = jnp.pad(jnp.cumsum(pf, axis=1), ((0, 0), (1, 0), (0, 0)))
    t = jnp.arange(S)
    outs = []
    for g, w in enumerate(POOL_WINDOWS):
        sl = slice(g * POOL_GROUP_DIM, (g + 1) * POOL_GROUP_DIM)
        hi = cs[:, 1:, sl]
        lo = jnp.pad(cs[:, :S - w + 1, sl], ((0, 0), (w - 1, 0), (0, 0)))
        count = jnp.minimum(t + 1, w).astype(jnp.float32)[None, :, None]
        outs.append((hi - lo) / count - pf[:, :, sl])
    pooled = jnp.stack(outs, axis=2).astype(p.dtype)
    mapped = jnp.einsum('bsgc,gcd->bsgd', pooled, pool_w).reshape(B, S, POOL_WIDTH)
    return mapped * pool_scale


def rope(x, positions):
    half = HEAD_DIM // 2
    inv = ROPE_THETA ** (-jnp.arange(half, dtype=jnp.float32) / half)
    ang = positions.astype(jnp.float32)[..., None] * inv
    cos = jnp.cos(ang)[:, :, None, :]
    sin = jnp.sin(ang)[:, :, None, :]
    xf = x.astype(jnp.float32)
    x1, x2 = xf[..., :half], xf[..., half:]
    return jnp.concatenate([x1 * cos - x2 * sin, x2 * cos + x1 * sin], axis=-1).astype(x.dtype)


def sliding_window_attention(q, k, v, sinks):
    B, S, _, _ = q.shape
    NB = S // WINDOW
    qb = q.reshape(B, NB, WINDOW, ATT_KV_HEADS, ATT_GROUP, HEAD_DIM)

    def band(t_):
        tb = t_.reshape(B, NB, WINDOW, ATT_KV_HEADS, HEAD_DIM)
        prev = jnp.pad(tb[:, :-1], ((0, 0), (1, 0), (0, 0), (0, 0), (0, 0)))
        return jnp.concatenate([prev, tb], axis=2)

    kb, vb = band(k), band(v)
    scores = jnp.einsum('bnqhgd,bnkhd->bnhgqk', qb, kb,
                        preferred_element_type=jnp.float32) * (HEAD_DIM ** -0.5)
    qi = jnp.arange(WINDOW)[:, None] + WINDOW
    ki = jnp.arange(2 * WINDOW)[None, :]
    rel = qi - ki
    valid = (rel >= 0) & (rel < WINDOW)
    valid = valid[None] & ((jnp.arange(NB) > 0)[:, None, None] | (ki >= WINDOW)[None])
    scores = jnp.where(valid[None, :, None, None], scores, -jnp.inf)
    sink = jnp.broadcast_to(
        sinks.astype(jnp.float32).reshape(ATT_KV_HEADS, ATT_GROUP)[None, None, :, :, None, None],
        scores.shape[:-1] + (1,))
    probs = jax.nn.softmax(jnp.concatenate([scores, sink], axis=-1), axis=-1)[..., :-1]
    out = jnp.einsum('bnhgqk,bnkhd->bnqhgd', probs.astype(v.dtype), vb)
    return out.reshape(B, S, ATT_WIDTH)


def cross_attention(h, mem_n, w_xq, w_xkv, w_xo):
    B, S, _ = h.shape
    M = mem_n.shape[1]
    q = (h @ w_xq).reshape(B, S, X_HEADS, X_HEAD_DIM)
    k, v = jnp.split(mem_n @ w_xkv, 2, axis=-1)
    k = k.reshape(B, M, X_HEADS, X_HEAD_DIM)
    v = v.reshape(B, M, X_HEADS, X_HEAD_DIM)
    s = jnp.einsum('bshd,bmhd->bhsm', q, k, preferred_element_type=jnp.float32) * (X_HEAD_DIM ** -0.5)
    p = jax.nn.softmax(s, axis=-1).astype(v.dtype)
    o = jnp.einsum('bhsm,bmhd->bshd', p, v).reshape(B, S, D_MODEL)
    return o @ w_xo


def setup_inputs(seed: int = 0) -> dict:
    key = jax.random.key(seed)
    ks = jax.random.split(key, 32)
    f32 = jnp.float32

    def nrm(k, shape, scale):
        return jax.random.normal(k, shape, f32) * scale

    def gain(k, shape):
        return 1.0 + 0.05 * jax.random.normal(k, shape, f32)

    offsets = jax.random.randint(ks[2], (BATCH, 1), 0, 1024, dtype=jnp.int32)
    positions = offsets + jnp.arange(SEQ, dtype=jnp.int32)[None, :]
    return {
        "x": nrm(ks[0], (BATCH, SEQ, D_MODEL), 1.0),
        "mem": nrm(ks[1], (BATCH, MEM_LEN, D_MODEL), 1.0),
        "positions": positions,
        "mem_norm_g": gain(ks[3], (D_MODEL,)),
        "mix_pre_g": gain(ks[4], (DEPTH, D_MODEL)),
        "mix_post_g": gain(ks[5], (DEPTH, D_MODEL)),
        "w_in": nrm(ks[6], (DEPTH, D_MODEL, D_IN), D_MODEL ** -0.5),
        "gm_v_g": gain(ks[7], (DEPTH, GM_WIDTH)),
        "gm_w_s": nrm(ks[8], (DEPTH, GM_GROUPS, CHUNK, CHUNK), 0.5 * CHUNK ** -0.5),
        "gm_b_s": 1.0 + 0.1 * jax.random.normal(ks[9], (DEPTH, GM_GROUPS, CHUNK), f32),
        "pool_w": nrm(ks[10], (DEPTH, POOL_GROUPS, POOL_GROUP_DIM, POOL_GROUP_DIM), POOL_GROUP_DIM ** -0.5),
        "pool_scale": 1.0 + 0.1 * jax.random.normal(ks[11], (DEPTH, POOL_WIDTH), f32),
        "attn_sinks": nrm(ks[12], (DEPTH, ATT_Q_HEADS), 1.0),
        "w_o": nrm(ks[13], (DEPTH, D_MIX, D_MODEL), D_MIX ** -0.5),
        "x_pre_g": gain(ks[14], (DEPTH, D_MODEL)),
        "x_post_g": gain(ks[15], (DEPTH, D_MODEL)),
        "w_xq": nrm(ks[16], (DEPTH, D_MODEL, D_MODEL), D_MODEL ** -0.5),
        "w_xkv": nrm(ks[17], (DEPTH, D_MODEL, 2 * D_MODEL), D_MODEL ** -0.5),
        "w_xo": nrm(ks[18], (DEPTH, D_MODEL, D_MODEL), D_MODEL ** -0.5),
        "ffn_pre_g": gain(ks[19], (DEPTH, D_MODEL)),
        "ffn_post_g": gain(ks[20], (DEPTH, D_MODEL)),
        "w_gate_up": nrm(ks[21], (DEPTH, D_MODEL, 2 * D_FF), D_MODEL ** -0.5),
        "w_down": nrm(ks[22], (DEPTH, D_FF, D_MODEL), D_FF ** -0.5),
    }


def reference(x, mem, positions, mem_norm_g, mix_pre_g, mix_post_g, w_in, gm_v_g, gm_w_s, gm_b_s,
              pool_w, pool_scale, attn_sinks, w_o, x_pre_g, x_post_g, w_xq, w_xkv, w_xo,
              ffn_pre_g, ffn_post_g, w_gate_up, w_down):
    B, S, _ = x.shape
    mem_n = rms_norm(mem, mem_norm_g)
    for l in range(DEPTH):
        h = rms_norm(x, mix_pre_g[l])
        z = h @ w_in[l]
        z_gm, z_pool, z_q, z_k, z_v = jnp.split(z, IN_SPLITS, axis=-1)
        a = spatial_gating(jax.nn.gelu(z_gm), gm_v_g[l], gm_w_s[l], gm_b_s[l])
        b = multiscale_pool(z_pool, pool_w[l], pool_scale[l])
        q = rope(z_q.reshape(B, S, ATT_Q_HEADS, HEAD_DIM), positions)
        k = rope(z_k.reshape(B, S, ATT_KV_HEADS, HEAD_DIM), positions)
        v = z_v.reshape(B, S, ATT_KV_HEADS, HEAD_DIM)
        c = sliding_window_attention(q, k, v, attn_sinks[l])
        mix = jnp.concatenate([a, b, c], axis=-1) @ w_o[l]
        x = x + rms_norm(mix, mix_post_g[l])
        h = rms_norm(x, x_pre_g[l])
        x = x + rms_norm(cross_attention(h, mem_n, w_xq[l], w_xkv[l], w_xo[l]), x_post_g[l])
        h = rms_norm(x, ffn_pre_g[l])
        gate, up = jnp.split(h @ w_gate_up[l], 2, axis=-1)
        f = (jax.nn.silu(gate) * up) @ w_down[l]
        x = x + rms_norm(f, ffn_post_g[l])
    return x
```

```python
import functools

import jax
import jax.numpy as jnp
import numpy as np
from jax import lax
from jax.experimental import pallas as pl
from jax.experimental.pallas import tpu as pltpu

D_MODEL = 1024
DEPTH = 4
MEM_LEN = 256
HEAD_DIM = 64
EPS = 1e-6
GM_WIDTH = 256
GM_GROUPS = 4
CHUNK = 128
POOL_WIDTH = 256
POOL_WINDOWS = (2, 4, 8, 16)
ATT_WIDTH = 512
ATT_Q_HEADS = 8
ATT_KV_HEADS = 2
WINDOW = 128
ROPE_THETA = 10000.0
D_IN = 1536
X_HEADS = 4
X_HEAD_DIM = 256
D_FF = 2816

OFF_POOL = 2 * GM_WIDTH
OFF_Q = OFF_POOL + POOL_WIDTH
OFF_K = OFF_Q + ATT_WIDTH
OFF_V = OFF_K + ATT_KV_HEADS * HEAD_DIM

LANES = 128
VMEM_LIMIT_BYTES = 56 * 1024 * 1024

TS_MIX = 512
TM_XATTN = 512
TM_FFN = 512
FF_CHUNK = 256
POOL_HALO = 16

NEG_BIG = -1e30
F32 = jnp.float32
BF16 = jnp.bfloat16


def _rms(xf, g):
    ms = jnp.mean(xf * xf, axis=-1, keepdims=True)
    return xf * lax.rsqrt(ms + EPS) * g


def _dot(a, b):
    return jnp.dot(a, b, preferred_element_type=F32)


def _dot_nt(a, b):
    return lax.dot_general(a, b, (((1,), (1,)), ((), ())), preferred_element_type=F32)


def _rope_table_kernel(pos_ref, inv_ref, cos_ref, sin_ref):
    ang = pos_ref[...].astype(F32) * inv_ref[...]
    lane = lax.broadcasted_iota(jnp.int32, ang.shape, 1)
    first_half = (lane % HEAD_DIM) < (HEAD_DIM // 2)
    cos_ref[...] = jnp.cos(ang)
    s = jnp.sin(ang)
    sin_ref[...] = jnp.where(first_half, -s, s)


def _rope_tables(positions):
    B, S = positions.shape
    half = HEAD_DIM // 2
    inv = ROPE_THETA ** (-jnp.arange(half, dtype=F32) / half)
    inv = jnp.tile(inv, LANES // half).reshape(1, LANES)
    T = 1024
    tok = pl.BlockSpec((None, T, 1), lambda b, s: (b, s, 0))
    out = pl.BlockSpec((None, T, LANES), lambda b, s: (b, s, 0))
    return pl.pallas_call(
        _rope_table_kernel,
        grid=(B, S // T),
        in_specs=[tok, pl.BlockSpec((1, LANES), lambda b, s: (0, 0))],
        out_specs=[out, out],
        out_shape=[jax.ShapeDtypeStruct((B, S, LANES), F32)] * 2,
        name="rope_tables",
    )(positions.reshape(B, S, 1), inv)


def _mem_kv_kernel(mem_ref, g_ref, w_ref, kv_ref):
    mem_n = _rms(mem_ref[...], g_ref[...]).astype(BF16)
    kv_ref[...] = _dot(mem_n, w_ref[...]).astype(BF16)


def _mem_kv(mem, mem_norm_g, w_xkv_bf16):
    B = mem.shape[0]
    return pl.pallas_call(
        _mem_kv_kernel,
        grid=(DEPTH, B),
        in_specs=[
            pl.BlockSpec((None, MEM_LEN, D_MODEL), lambda l, b: (b, 0, 0)),
            pl.BlockSpec((1, D_MODEL), lambda l, b: (0, 0)),
            pl.BlockSpec((None, D_MODEL, 2 * D_MODEL), lambda l, b: (l, 0, 0)),
        ],
        out_specs=pl.BlockSpec((None, None, MEM_LEN, 2 * D_MODEL), lambda l, b: (l, b, 0, 0)),
        out_shape=jax.ShapeDtypeStruct((DEPTH, B, MEM_LEN, 2 * D_MODEL), BF16),
        compiler_params=pltpu.CompilerParams(vmem_limit_bytes=VMEM_LIMIT_BYTES),
        name="mem_kv",
    )(mem, mem_norm_g.reshape(1, D_MODEL), w_xkv_bf16)


def _rope_cols(xc, cos_t, sin_t, first_half):
    rot = jnp.where(first_half, pltpu.roll(xc, LANES - HEAD_DIM // 2, 1), pltpu.roll(xc, HEAD_DIM // 2, 1))
    return xc * cos_t + rot * sin_t


def _mix_kernel(layer, sinks_ref, x_ref, cos_ref, sin_ref, gpre_ref, win_ref, gvg_ref, ws_ref, bs_ref,
                pw_ref, ps_ref, wo_ref, gpost_ref, o_ref,
                q_scr, klo_scr, khi_scr, vlo_scr, vhi_scr, pool_scr, mix_scr):
    ts = x_ref.shape[0]
    n_blocks = ts // WINDOW
    s_idx = pl.program_id(1)

    @pl.when(s_idx == 0)
    def _():
        for scr in (klo_scr, khi_scr, vlo_scr, vhi_scr):
            scr[:, 0:WINDOW, :] = jnp.zeros((ATT_KV_HEADS, WINDOW, LANES), BF16)
        pool_scr[0:POOL_HALO, :] = jnp.zeros((POOL_HALO, POOL_WIDTH), F32)

    @pl.when(s_idx > 0)
    def _():
        for scr in (klo_scr, khi_scr, vlo_scr, vhi_scr):
            scr[:, 0:WINDOW, :] = scr[:, ts:ts + WINDOW, :]
        pool_scr[0:POOL_HALO, :] = pool_scr[ts:ts + POOL_HALO, :]

    x = x_ref[...]
    h = _rms(x, gpre_ref[...]).astype(BF16)
    z = _dot(h, win_ref[...])

    lane = lax.broadcasted_iota(jnp.int32, (1, LANES), 1)
    lane_lo = lane < HEAD_DIM
    first_half = (lane % HEAD_DIM) < (HEAD_DIM // 2)
    lane_w = lax.broadcasted_iota(jnp.int32, (1, GM_WIDTH), 1)
    group_w = lane_w // HEAD_DIM

    zg = jax.nn.gelu(z[:, 0:2 * GM_WIDTH])
    u = zg[:, 0:GM_WIDTH]
    v = zg[:, GM_WIDTH:2 * GM_WIDTH]
    vv = v * v
    ms = jnp.zeros_like(vv)
    for g in range(GM_GROUPS):
        in_g = group_w == g
        sg = jnp.sum(jnp.where(in_g, vv, 0.0), axis=-1, keepdims=True)
        ms = jnp.where(in_g, sg, ms)
    vn = v * lax.rsqrt(ms * (1.0 / HEAD_DIM) + EPS) * gvg_ref[...]
    t_i = lax.broadcasted_iota(jnp.int32, (CHUNK, GM_GROUPS * CHUNK), 0)
    s_i = lax.broadcasted_iota(jnp.int32, (CHUNK, GM_GROUPS * CHUNK), 1) % CHUNK
    w_cat = jnp.where(t_i >= s_i, ws_ref[...], 0.0).astype(BF16)
    bias = bs_ref[...]
    for c in range(ts // CHUNK):
        rows = slice(c * CHUNK, (c + 1) * CHUNK)
        vc = vn[rows, :]
        r_op = jnp.concatenate([jnp.where(group_w == g, vc, 0.0) for g in range(GM_GROUPS)], axis=0).astype(BF16)
        mixed = _dot(w_cat, r_op) + bias
        mix_scr[rows, 0:GM_WIDTH] = (u[rows, :] * mixed).astype(BF16)

    zp = z[:, OFF_POOL:OFF_POOL + POOL_WIDTH]
    pool_scr[POOL_HALO:POOL_HALO + ts, :] = zp
    p_ext = pool_scr[...]
    run = p_ext
    sums = []
    shift = 1
    for _ in POOL_WINDOWS:
        run = run + pltpu.roll(run, shift, 0)
        sums.append(run)
        shift *= 2
    win_sum = sums[-1]
    for g in range(len(POOL_WINDOWS) - 2, -1, -1):
        win_sum = jnp.where(group_w == g, sums[g], win_sum)
    win_sum = win_sum[POOL_HALO:POOL_HALO + ts, :]
    t_glob = s_idx * ts + lax.broadcasted_iota(jnp.int32, (ts, 1), 0)
    width = jnp.full((1, POOL_WIDTH), POOL_WINDOWS[-1], jnp.int32)
    for g in range(len(POOL_WINDOWS) - 2, -1, -1):
        width = jnp.where(group_w == g, POOL_WINDOWS[g], width)
    count = jnp.minimum(t_glob + 1, width).astype(F32)
    pooled = win_sum / count - zp
    mix_scr[:, GM_WIDTH:GM_WIDTH + POOL_WIDTH] = (_dot(pooled.astype(BF16), pw_ref[...]) * ps_ref[...]).astype(BF16)

    cos_t = cos_ref[...]
    sin_t = sin_ref[...]
    scale = HEAD_DIM ** -0.5
    for j in range(ATT_WIDTH // LANES):
        qc = _rope_cols(z[:, OFF_Q + j * LANES:OFF_Q + (j + 1) * LANES], cos_t, sin_t, first_half)
        q_scr[:, j * LANES:(j + 1) * LANES] = (qc * scale).astype(BF16)
    k_r = _rope_cols(z[:, OFF_K:OFF_K + LANES], cos_t, sin_t, first_half)
    k_x = pltpu.roll(k_r, HEAD_DIM, 1)
    v_r = z[:, OFF_V:OFF_V + LANES]
    v_x = pltpu.roll(v_r, HEAD_DIM, 1)
    new = slice(WINDOW, WINDOW + ts)
    klo_scr[0, new, :] = jnp.where(lane_lo, k_r, 0.0).astype(BF16)
    khi_scr[0, new, :] = jnp.where(lane_lo, 0.0, k_x).astype(BF16)
    klo_scr[1, new, :] = jnp.where(lane_lo, k_x, 0.0).astype(BF16)
    khi_scr[1, new, :] = jnp.where(lane_lo, 0.0, k_r).astype(BF16)
    vlo_scr[0, new, :] = jnp.where(lane_lo, v_r, 0.0).astype(BF16)
    vhi_scr[0, new, :] = jnp.where(lane_lo, 0.0, v_x).astype(BF16)
    vlo_scr[1, new, :] = jnp.where(lane_lo, v_x, 0.0).astype(BF16)
    vhi_scr[1, new, :] = jnp.where(lane_lo, 0.0, v_r).astype(BF16)

    qi = lax.broadcasted_iota(jnp.int32, (WINDOW, 2 * WINDOW), 0) + WINDOW
    ki = lax.broadcasted_iota(jnp.int32, (WINDOW, 2 * WINDOW), 1)
    rel = qi - ki
    in_window = (rel >= 0) & (rel < WINDOW)

    def block_body(n, carry):
        r0 = pl.multiple_of(n * WINDOW, WINDOW)
        valid = in_window & ((ki + (s_idx * n_blocks + n) * WINDOW) >= WINDOW)
        for j in range(ATT_WIDTH // LANES):
            kvh = (2 * j) // (ATT_Q_HEADS // ATT_KV_HEADS)
            qc = q_scr[pl.ds(r0, WINDOW), j * LANES:(j + 1) * LANES]
            keys = pl.ds(r0, 2 * WINDOW)
            probs = []
            inv_l = []
            for parity, k_scr in enumerate((klo_scr, khi_scr)):
                sink = sinks_ref[layer, 2 * j + parity]
                sc = jnp.where(valid, _dot_nt(qc, k_scr[kvh, keys, :]), NEG_BIG)
                m = jnp.maximum(jnp.max(sc, axis=-1, keepdims=True), sink)
                p = jnp.exp(sc - m)
                l = jnp.sum(p, axis=-1, keepdims=True) + jnp.exp(sink - m)
                probs.append(p.astype(BF16))
                inv_l.append(1.0 / l)
            o = _dot(probs[0], vlo_scr[kvh, keys, :]) + _dot(probs[1], vhi_scr[kvh, keys, :])
            o = o * jnp.where(lane_lo, inv_l[0], inv_l[1])
            c0 = GM_WIDTH + POOL_WIDTH + j * LANES
            mix_scr[pl.ds(r0, WINDOW), c0:c0 + LANES] = o.astype(BF16)
        return carry

    lax.fori_loop(0, n_blocks, block_body, 0)

    mix = _dot(mix_scr[...], wo_ref[...])
    o_ref[...] = x + _rms(mix, gpost_ref[...])


def _mix_layer(layer, x, cos_t, sin_t, sinks, gpre, w_in, gvg, ws_cat, bs_tile, pw_bd, ps, w_o, gpost):
    B, S, D = x.shape
    ts = TS_MIX
    tok = lambda w: pl.BlockSpec((None, ts, w), lambda b, s: (b, s, 0))
    par = lambda *shape: pl.BlockSpec((None,) + shape, lambda b, s: (layer,) + (0,) * len(shape))
    return pl.pallas_call(
        functools.partial(_mix_kernel, layer),
        grid=(B, S // ts),
        in_specs=[
            pl.BlockSpec(memory_space=pltpu.SMEM),
            tok(D), tok(LANES), tok(LANES),
            par(1, D), par(D, D_IN), par(1, GM_WIDTH), par(CHUNK, GM_GROUPS * CHUNK), par(CHUNK, GM_WIDTH),
            par(POOL_WIDTH, POOL_WIDTH), par(1, POOL_WIDTH), par(D, D), par(1, D),
        ],
        out_specs=tok(D),
        out_shape=jax.ShapeDtypeStruct((B, S, D), F32),
        scratch_shapes=[
            pltpu.VMEM((ts, ATT_WIDTH), BF16),
            pltpu.VMEM((ATT_KV_HEADS, WINDOW + ts, LANES), BF16),
            pltpu.VMEM((ATT_KV_HEADS, WINDOW + ts, LANES), BF16),
            pltpu.VMEM((ATT_KV_HEADS, WINDOW + ts, LANES), BF16),
            pltpu.VMEM((ATT_KV_HEADS, WINDOW + ts, LANES), BF16),
            pltpu.VMEM((POOL_HALO + ts, POOL_WIDTH), F32),
            pltpu.VMEM((ts, D), BF16),
        ],
        compiler_params=pltpu.CompilerParams(
            dimension_semantics=("arbitrary", "arbitrary"), vmem_limit_bytes=VMEM_LIMIT_BYTES),
        name=f"mix_l{layer}",
    )(sinks, x, cos_t, sin_t, gpre, w_in, gvg, ws_cat, bs_tile, pw_bd, ps, w_o, gpost)


def _xattn_kernel(x_ref, gpre_ref, wq_ref, kv_ref, wo_ref, gpost_ref, o_ref, q_scr, o_scr):
    x = x_ref[...]
    h = _rms(x, gpre_ref[...]).astype(BF16)
    q_scr[...] = (_dot(h, wq_ref[...]) * (X_HEAD_DIM ** -0.5)).astype(BF16)
    for hd in range(X_HEADS):
        cols = slice(hd * X_HEAD_DIM, (hd + 1) * X_HEAD_DIM)
        k_h = kv_ref[:, hd * X_HEAD_DIM:(hd + 1) * X_HEAD_DIM]
        v_h = kv_ref[:, D_MODEL + hd * X_HEAD_DIM:D_MODEL + (hd + 1) * X_HEAD_DIM]
        sc = _dot_nt(q_scr[:, cols], k_h)
        m = jnp.max(sc, axis=-1, keepdims=True)
        p = jnp.exp(sc - m)
        l = jnp.sum(p, axis=-1, keepdims=True)
        o_scr[:, cols] = (_dot(p.astype(BF16), v_h) * (1.0 / l)).astype(BF16)
    out = _dot(o_scr[...], wo_ref[...])
    o_ref[...] = x + _rms(out, gpost_ref[...])


def _xattn_layer(layer, x, gpre, w_xq, kv, w_xo, gpost):
    B, S, D = x.shape
    tm = TM_XATTN
    tok = pl.BlockSpec((None, tm, D), lambda b, s: (b, s, 0))
    par = lambda *shape: pl.BlockSpec((None,) + shape, lambda b, s: (layer,) + (0,) * len(shape))
    return pl.pallas_call(
        _xattn_kernel,
        grid=(B, S // tm),
        in_specs=[
            tok, par(1, D), par(D, D),
            pl.BlockSpec((None, None, MEM_LEN, 2 * D), lambda b, s: (layer, b, 0, 0)),
            par(D, D), par(1, D),
        ],
        out_specs=tok,
        out_shape=jax.ShapeDtypeStruct((B, S, D), F32),
        scratch_shapes=[pltpu.VMEM((tm, D), BF16), pltpu.VMEM((tm, D), BF16)],
        compiler_params=pltpu.CompilerParams(
            dimension_semantics=("arbitrary", "arbitrary"), vmem_limit_bytes=VMEM_LIMIT_BYTES),
        name=f"xattn_l{layer}",
    )(x, gpre, w_xq, kv, w_xo, gpost)


def _ffn_kernel(x_ref, gpre_ref, wgu_ref, wd_ref, gpost_ref, o_ref, act_scr):
    x = x_ref[...]
    h = _rms(x, gpre_ref[...]).astype(BF16)
    for c in range(0, D_FF, FF_CHUNK):
        gate = _dot(h, wgu_ref[:, c:c + FF_CHUNK])
        up = _dot(h, wgu_ref[:, D_FF + c:D_FF + c + FF_CHUNK])
        act_scr[:, c:c + FF_CHUNK] = (jax.nn.silu(gate) * up).astype(BF16)
    f = _dot(act_scr[...], wd_ref[...])
    o_ref[...] = x + _rms(f, gpost_ref[...])


def _ffn_layer(layer, x, gpre, w_gu, w_d, gpost):
    B, S, D = x.shape
    tm = TM_FFN
    tok = pl.BlockSpec((None, tm, D), lambda b, s: (b, s, 0))
    par = lambda *shape: pl.BlockSpec((None,) + shape, lambda b, s: (layer,) + (0,) * len(shape))
    return pl.pallas_call(
        _ffn_kernel,
        grid=(B, S // tm),
        in_specs=[tok, par(1, D), par(D, 2 * D_FF), par(D_FF, D), par(1, D)],
        out_specs=tok,
        out_shape=jax.ShapeDtypeStruct((B, S, D), F32),
        scratch_shapes=[pltpu.VMEM((tm, D_FF), BF16)],
        compiler_params=pltpu.CompilerParams(
            dimension_semantics=("arbitrary", "arbitrary"), vmem_limit_bytes=VMEM_LIMIT_BYTES),
        name=f"ffn_l{layer}",
    )(x, gpre, w_gu, w_d, gpost)


def kernel(x, mem, positions, mem_norm_g, mix_pre_g, mix_post_g, w_in, gm_v_g, gm_w_s, gm_b_s, pool_w, pool_scale,
           attn_sinks, w_o, x_pre_g, x_post_g, w_xq, w_xkv, w_xo, ffn_pre_g, ffn_post_g, w_gate_up, w_down):
    assert x.shape[1] % TS_MIX == 0 and x.shape[1] % TM_XATTN == 0 and x.shape[1] % TM_FFN == 0
    row = lambda g: g.reshape(DEPTH, 1, -1)
    w_in_b, w_o_b = w_in.astype(BF16), w_o.astype(BF16)
    w_xq_b, w_xkv_b, w_xo_b = w_xq.astype(BF16), w_xkv.astype(BF16), w_xo.astype(BF16)
    w_gu_b, w_d_b = w_gate_up.astype(BF16), w_down.astype(BF16)
    ws_cat = jnp.transpose(gm_w_s, (0, 2, 1, 3)).reshape(DEPTH, CHUNK, GM_GROUPS * CHUNK)
    bs_tile = jnp.repeat(jnp.transpose(gm_b_s, (0, 2, 1)), HEAD_DIM, axis=2)
    n_pg = len(POOL_WINDOWS)
    pw_bd = jnp.einsum('lgcd,gh->lgchd', pool_w, jnp.eye(n_pg, dtype=pool_w.dtype))
    pw_bd = pw_bd.reshape(DEPTH, POOL_WIDTH, POOL_WIDTH).astype(BF16)

    cos_t, sin_t = _rope_tables(positions)
    kv = _mem_kv(mem, mem_norm_g, w_xkv_b)
    for l in range(DEPTH):
        x = _mix_layer(l, x, cos_t, sin_t, attn_sinks, row(mix_pre_g), w_in_b, row(gm_v_g), ws_cat, bs_tile,
                       pw_bd, row(pool_scale), w_o_b, row(mix_post_g))
        x = _xattn_layer(l, x, row(x_pre_g), w_xq_b, kv, w_xo_b, row(x_post_g))
        x = _ffn_layer(l, x, row(ffn_pre_g), w_gu_b, w_d_b, row(ffn_post_g))
    return x
```

```python
import functools

import jax
import jax.numpy as jnp
from jax import lax
from jax.experimental import pallas as pl
from jax.experimental.pallas import tpu as pltpu

D_MODEL = 1024
DEPTH = 4
MEM_LEN = 256
HEAD_DIM = 64
HALF_DIM = HEAD_DIM // 2
EPS = 1e-6
GM_WIDTH = 256
GM_GROUPS = 4
CHUNK = 128
POOL_WIDTH = 256
POOL_WINDOWS = (2, 4, 8, 16)
ATT_WIDTH = 512
ATT_Q_HEADS = 8
ATT_KV_HEADS = 2
ATT_GROUP = ATT_Q_HEADS // ATT_KV_HEADS
WINDOW = 128
ROPE_THETA = 10000.0
D_IN = 1536
X_HEADS = 4
X_HEAD_DIM = 256
D_FF = 2816

OFF_POOL = 2 * GM_WIDTH
OFF_Q = OFF_POOL + POOL_WIDTH
OFF_K = OFF_Q + ATT_WIDTH
OFF_V = OFF_K + ATT_KV_HEADS * HEAD_DIM

LANES = 128
VMEM_LIMIT_BYTES = 56 * 1024 * 1024

SLOTS = LANES // HALF_DIM
PAIRS = ATT_Q_HEADS // SLOTS
PAIR_WIDTH = 2 * LANES

TS_MIX = 512
TM_XATTN = 512
TM_FFN = 512
FF_CHUNK = 256
POOL_HALO = 16

NEG_BIG = -1e30
F32 = jnp.float32
BF16 = jnp.bfloat16


def _rms(xf, g):
    ms = jnp.mean(xf * xf, axis=-1, keepdims=True)
    return xf * lax.rsqrt(ms + EPS) * g


def _dot(a, b):
    return jnp.dot(a, b, preferred_element_type=F32)


def _dot_nt(a, b):
    return lax.dot_general(a, b, (((1,), (1,)), ((), ())), preferred_element_type=F32)


def _rope_table_kernel(pos_ref, inv_ref, cos_ref, sin_ref):
    ang = pos_ref[...].astype(F32) * inv_ref[...]
    cos_ref[...] = jnp.cos(ang)
    sin_ref[...] = jnp.sin(ang)


def _rope_tables(positions):
    B, S = positions.shape
    inv = ROPE_THETA ** (-jnp.arange(HALF_DIM, dtype=F32) / HALF_DIM)
    inv = jnp.tile(inv, LANES // HALF_DIM).reshape(1, LANES)
    T = 1024
    tok = pl.BlockSpec((None, T, 1), lambda b, s: (b, s, 0))
    out = pl.BlockSpec((None, T, LANES), lambda b, s: (b, s, 0))
    return pl.pallas_call(
        _rope_table_kernel,
        grid=(B, S // T),
        in_specs=[tok, pl.BlockSpec((1, LANES), lambda b, s: (0, 0))],
        out_specs=[out, out],
        out_shape=[jax.ShapeDtypeStruct((B, S, LANES), F32)] * 2,
        name="rope_tables",
    )(positions.reshape(B, S, 1), inv)


def _mem_kv_kernel(mem_ref, g_ref, w_ref, kv_ref):
    mem_n = _rms(mem_ref[...], g_ref[...]).astype(BF16)
    kv_ref[...] = _dot(mem_n, w_ref[...]).astype(BF16)


def _mem_kv(mem, mem_norm_g, w_xkv_bf16):
    B = mem.shape[0]
    return pl.pallas_call(
        _mem_kv_kernel,
        grid=(DEPTH, B),
        in_specs=[
            pl.BlockSpec((None, MEM_LEN, D_MODEL), lambda l, b: (b, 0, 0)),
            pl.BlockSpec((1, D_MODEL), lambda l, b: (0, 0)),
            pl.BlockSpec((None, D_MODEL, 2 * D_MODEL), lambda l, b: (l, 0, 0)),
        ],
        out_specs=pl.BlockSpec((None, None, MEM_LEN, 2 * D_MODEL), lambda l, b: (l, b, 0, 0)),
        out_shape=jax.ShapeDtypeStruct((DEPTH, B, MEM_LEN, 2 * D_MODEL), BF16),
        compiler_params=pltpu.CompilerParams(vmem_limit_bytes=VMEM_LIMIT_BYTES),
        name="mem_kv",
    )(mem, mem_norm_g.reshape(1, D_MODEL), w_xkv_bf16)


def _mix_kernel(layer, sinks_ref, x_ref, cos_ref, sin_ref, gpre_ref, win_ref, gvg_ref, ones_ref, ws_ref, bs_ref,
                pw_ref, ps_ref, wo_ref, gpost_ref, o_ref,
                q_scr, k_scr, v_scr, pool_scr, mix_scr):
    ts = x_ref.shape[0]
    n_blocks = ts // WINDOW
    s_idx = pl.program_id(1)
    new = slice(WINDOW, WINDOW + ts)

    @pl.when(s_idx == 0)
    def _():
        k_scr[0:WINDOW, :] = jnp.zeros((WINDOW, PAIR_WIDTH), BF16)
        v_scr[0:WINDOW, :] = jnp.zeros((WINDOW, PAIR_WIDTH), BF16)
        pool_scr[0:POOL_HALO, :] = jnp.zeros((POOL_HALO, POOL_WIDTH), F32)

    @pl.when(s_idx > 0)
    def _():
        k_scr[0:WINDOW, :] = k_scr[ts:ts + WINDOW, :]
        v_scr[0:WINDOW, :] = v_scr[ts:ts + WINDOW, :]
        pool_scr[0:POOL_HALO, :] = pool_scr[ts:ts + POOL_HALO, :]

    x = x_ref[...]
    h = _rms(x, gpre_ref[...]).astype(BF16)
    z = _dot(h, win_ref[...])

    lane = lax.broadcasted_iota(jnp.int32, (1, LANES), 1)
    lane_lo = lane < HEAD_DIM
    group_w = lax.broadcasted_iota(jnp.int32, (1, GM_WIDTH), 1) // HEAD_DIM

    zg = jax.nn.gelu(z[:, 0:2 * GM_WIDTH])
    u = zg[:, 0:GM_WIDTH]
    v = zg[:, GM_WIDTH:2 * GM_WIDTH]
    vv = v * v
    vv_hi = vv.astype(BF16)
    vv_lo = (vv - vv_hi.astype(F32)).astype(BF16)
    ms = (_dot(vv_hi, ones_ref[...]) + _dot(vv_lo, ones_ref[...])) * (1.0 / HEAD_DIM)
    vn = (v * lax.rsqrt(ms + EPS) * gvg_ref[...]).astype(BF16)
    t_i = lax.broadcasted_iota(jnp.int32, (CHUNK, GM_GROUPS * CHUNK), 0)
    s_i = lax.broadcasted_iota(jnp.int32, (CHUNK, GM_GROUPS * CHUNK), 1) % CHUNK
    w_cat = jnp.where(t_i >= s_i, ws_ref[...], 0.0).astype(BF16)
    bias = bs_ref[...]
    group_full = lax.broadcasted_iota(jnp.int32, (CHUNK, GM_WIDTH), 1) // HEAD_DIM
    zero_b = jnp.zeros((CHUNK, GM_WIDTH), BF16)
    for c in range(ts // CHUNK):
        rows = slice(c * CHUNK, (c + 1) * CHUNK)
        vc = vn[rows, :]
        r_op = jnp.concatenate([jnp.where(group_full == g, vc, zero_b) for g in range(GM_GROUPS)], axis=0)
        mixed = _dot(w_cat, r_op) + bias
        mix_scr[rows, 0:GM_WIDTH] = (u[rows, :] * mixed).astype(BF16)

    zp = z[:, OFF_POOL:OFF_POOL + POOL_WIDTH]
    pool_scr[POOL_HALO:POOL_HALO + ts, :] = zp
    run = pool_scr[...]
    sums = []
    shift = 1
    for _ in POOL_WINDOWS:
        run = run + pltpu.roll(run, shift, 0)
        sums.append(run)
        shift *= 2
    win_sum = sums[-1]
    width = jnp.full((1, POOL_WIDTH), POOL_WINDOWS[-1], jnp.int32)
    for g in range(len(POOL_WINDOWS) - 2, -1, -1):
        win_sum = jnp.where(group_w == g, sums[g], win_sum)
        width = jnp.where(group_w == g, POOL_WINDOWS[g], width)
    win_sum = win_sum[POOL_HALO:POOL_HALO + ts, :]
    t_head = s_idx * ts + lax.broadcasted_iota(jnp.int32, (POOL_HALO, 1), 0)
    count_head = jnp.minimum(t_head + 1, width).astype(F32)
    pooled = jnp.concatenate([
        win_sum[0:POOL_HALO, :] / count_head - zp[0:POOL_HALO, :],
        win_sum[POOL_HALO:, :] * (1.0 / width.astype(F32)) - zp[POOL_HALO:, :]], axis=0)
    mix_scr[:, GM_WIDTH:GM_WIDTH + POOL_WIDTH] = (_dot(pooled.astype(BF16), pw_ref[...]) * ps_ref[...]).astype(BF16)

    cos_t = cos_ref[...]
    sin_t = sin_ref[...]
    scale = HEAD_DIM ** -0.5
    slot_of_lane = (lax.broadcasted_iota(jnp.int32, (ts, PAIR_WIDTH), 1) % LANES) // HALF_DIM
    zero_q = jnp.zeros((ts, PAIR_WIDTH), BF16)
    for p in range(PAIRS):
        a = z[:, OFF_Q + p * PAIR_WIDTH:OFF_Q + p * PAIR_WIDTH + LANES]
        b = z[:, OFF_Q + p * PAIR_WIDTH + LANES:OFF_Q + (p + 1) * PAIR_WIDTH]
        q_ab = jnp.concatenate([(a * cos_t - b * sin_t) * scale, (b * cos_t + a * sin_t) * scale], axis=1).astype(BF16)
        for i in range(SLOTS):
            q_scr[p, i] = jnp.where(slot_of_lane == i, q_ab, zero_q)
    k_z = z[:, OFF_K:OFF_K + LANES]
    k_r = k_z * cos_t + pltpu.roll(k_z, HEAD_DIM, 1) * jnp.where(lane_lo, -sin_t, sin_t)
    k_x = pltpu.roll(k_r, HEAD_DIM, 1)
    k_scr[new, 0:LANES] = jnp.where(lane_lo, k_r, k_x).astype(BF16)
    k_scr[new, LANES:PAIR_WIDTH] = jnp.where(lane_lo, k_x, k_r).astype(BF16)
    v_scr[new, 0:LANES] = z[:, OFF_V:OFF_V + LANES].astype(BF16)
    v_scr[new, LANES:PAIR_WIDTH] = jnp.ones((ts, LANES), BF16)

    q_i = lax.broadcasted_iota(jnp.int32, (WINDOW, WINDOW), 0)
    c_i = lax.broadcasted_iota(jnp.int32, (WINDOW, WINDOW), 1)
    use_prev = c_i > q_i
    zero_p = jnp.zeros((WINDOW, WINDOW), BF16)
    for n in range(n_blocks):
        rows = slice(n * WINDOW, (n + 1) * WINDOW)
        keys = slice(n * WINDOW, (n + 2) * WINDOW)
        k_blk = k_scr[keys, :]
        v_blk = v_scr[keys, :]
        for p in range(PAIRS):
            sc = _dot_nt(q_scr[p, :, rows, :].reshape(SLOTS * WINDOW, PAIR_WIDTH), k_blk)
            probs = []
            sink_terms = []
            for i in range(SLOTS):
                sc_i = sc[i * WINDOW:(i + 1) * WINDOW, :]
                f = jnp.where(use_prev, sc_i[:, 0:WINDOW], sc_i[:, WINDOW:2 * WINDOW])
                if n == 0:
                    f = jnp.where((q_i - c_i + s_idx * ts) >= 0, f, NEG_BIG)
                sink = sinks_ref[layer, p * SLOTS + i]
                m = jnp.maximum(jnp.max(f, axis=-1, keepdims=True), sink)
                pe = jnp.exp(f - m).astype(BF16)
                probs.append(jnp.concatenate([jnp.where(use_prev, pe, zero_p), jnp.where(use_prev, zero_p, pe)], axis=1))
                sink_terms.append(jnp.exp(sink - m))
            o = _dot(jnp.concatenate(probs, axis=0), v_blk)
            outs = []
            for i in range(SLOTS):
                o_i = o[i * WINDOW:(i + 1) * WINDOW, :]
                outs.append(o_i[:, 0:LANES] * (1.0 / (o_i[:, LANES:PAIR_WIDTH] + sink_terms[i])))
            c0 = GM_WIDTH + POOL_WIDTH + p * PAIR_WIDTH
            mix_scr[rows, c0:c0 + LANES] = jnp.where(lane_lo, outs[0], outs[1]).astype(BF16)
            mix_scr[rows, c0 + LANES:c0 + PAIR_WIDTH] = jnp.where(lane_lo, outs[2], outs[3]).astype(BF16)

    mix = _dot(mix_scr[...], wo_ref[...])
    o_ref[...] = x + _rms(mix, gpost_ref[...])


def _mix_layer(layer, x, cos_t, sin_t, sinks, gpre, w_in, gvg, ones_bd, ws_cat, bs_tile, pw_bd, ps, w_o, gpost):
    B, S, D = x.shape
    ts = TS_MIX
    tok = lambda w: pl.BlockSpec((None, ts, w), lambda b, s: (b, s, 0))
    par = lambda *shape: pl.BlockSpec((None,) + shape, lambda b, s: (layer,) + (0,) * len(shape))
    return pl.pallas_call(
        functools.partial(_mix_kernel, layer),
        grid=(B, S // ts),
        in_specs=[
            pl.BlockSpec(memory_space=pltpu.SMEM),
            tok(D), tok(LANES), tok(LANES),
            par(1, D), par(D, D_IN), par(1, GM_WIDTH),
            pl.BlockSpec((GM_WIDTH, GM_WIDTH), lambda b, s: (0, 0)),
            par(CHUNK, GM_GROUPS * CHUNK), par(CHUNK, GM_WIDTH),
            par(POOL_WIDTH, POOL_WIDTH), par(1, POOL_WIDTH), par(D, D), par(1, D),
        ],
        out_specs=tok(D),
        out_shape=jax.ShapeDtypeStruct((B, S, D), F32),
        scratch_shapes=[
            pltpu.VMEM((PAIRS, SLOTS, ts, PAIR_WIDTH), BF16),
            pltpu.VMEM((WINDOW + ts, PAIR_WIDTH), BF16),
            pltpu.VMEM((WINDOW + ts, PAIR_WIDTH), BF16),
            pltpu.VMEM((POOL_HALO + ts, POOL_WIDTH), F32),
            pltpu.VMEM((ts, D), BF16),
        ],
        compiler_params=pltpu.CompilerParams(
            dimension_semantics=("arbitrary", "arbitrary"), vmem_limit_bytes=VMEM_LIMIT_BYTES),
        name=f"mix_l{layer}",
    )(sinks, x, cos_t, sin_t, gpre, w_in, gvg, ones_bd, ws_cat, bs_tile, pw_bd, ps, w_o, gpost)


def _xattn_kernel(x_ref, gpre_ref, wq_ref, kv_ref, wo_ref, gpost_ref, o_ref, q_scr, o_scr):
    x = x_ref[...]
    h = _rms(x, gpre_ref[...]).astype(BF16)
    q_scr[...] = (_dot(h, wq_ref[...]) * (X_HEAD_DIM ** -0.5)).astype(BF16)
    for hd in range(X_HEADS):
        cols = slice(hd * X_HEAD_DIM, (hd + 1) * X_HEAD_DIM)
        k_h = kv_ref[:, hd * X_HEAD_DIM:(hd + 1) * X_HEAD_DIM]
        v_h = kv_ref[:, D_MODEL + hd * X_HEAD_DIM:D_MODEL + (hd + 1) * X_HEAD_DIM]
        sc = _dot_nt(q_scr[:, cols], k_h)
        m = jnp.max(sc, axis=-1, keepdims=True)
        p = jnp.exp(sc - m)
        l = jnp.sum(p, axis=-1, keepdims=True)
        o_scr[:, cols] = (_dot(p.astype(BF16), v_h) * (1.0 / l)).astype(BF16)
    out = _dot(o_scr[...], wo_ref[...])
    o_ref[...] = x + _rms(out, gpost_ref[...])


def _xattn_layer(layer, x, gpre, w_xq, kv, w_xo, gpost):
    B, S, D = x.shape
    tm = TM_XATTN
    tok = pl.BlockSpec((None, tm, D), lambda b, s: (b, s, 0))
    par = lambda *shape: pl.BlockSpec((None,) + shape, lambda b, s: (layer,) + (0,) * len(shape))
    return pl.pallas_call(
        _xattn_kernel,
        grid=(B, S // tm),
        in_specs=[
            tok, par(1, D), par(D, D),
            pl.BlockSpec((None, None, MEM_LEN, 2 * D), lambda b, s: (layer, b, 0, 0)),
            par(D, D), par(1, D),
        ],
        out_specs=tok,
        out_shape=jax.ShapeDtypeStruct((B, S, D), F32),
        scratch_shapes=[pltpu.VMEM((tm, D), BF16), pltpu.VMEM((tm, D), BF16)],
        compiler_params=pltpu.CompilerParams(
            dimension_semantics=("arbitrary", "arbitrary"), vmem_limit_bytes=VMEM_LIMIT_BYTES),
        name=f"xattn_l{layer}",
    )(x, gpre, w_xq, kv, w_xo, gpost)


def _ffn_kernel(x_ref, gpre_ref, wgu_ref, wd_ref, gpost_ref, o_ref, act_scr):
    x = x_ref[...]
    h = _rms(x, gpre_ref[...]).astype(BF16)
    for c in range(0, D_FF, FF_CHUNK):
        gate = _dot(h, wgu_ref[:, c:c + FF_CHUNK])
        up = _dot(h, wgu_ref[:, D_FF + c:D_FF + c + FF_CHUNK])
        act_scr[:, c:c + FF_CHUNK] = (jax.nn.silu(gate) * up).astype(BF16)
    f = _dot(act_scr[...], wd_ref[...])
    o_ref[...] = x + _rms(f, gpost_ref[...])


def _ffn_layer(layer, x, gpre, w_gu, w_d, gpost):
    B, S, D = x.shape
    tm = TM_FFN
    tok = pl.BlockSpec((None, tm, D), lambda b, s: (b, s, 0))
    par = lambda *shape: pl.BlockSpec((None,) + shape, lambda b, s: (layer,) + (0,) * len(shape))
    return pl.pallas_call(
        _ffn_kernel,
        grid=(B, S // tm),
        in_specs=[tok, par(1, D), par(D, 2 * D_FF), par(D_FF, D), par(1, D)],
        out_specs=tok,
        out_shape=jax.ShapeDtypeStruct((B, S, D), F32),
        scratch_shapes=[pltpu.VMEM((tm, D_FF), BF16)],
        compiler_params=pltpu.CompilerParams(
            dimension_semantics=("arbitrary", "arbitrary"), vmem_limit_bytes=VMEM_LIMIT_BYTES),
        name=f"ffn_l{layer}",
    )(x, gpre, w_gu, w_d, gpost)


def _attention_layouts(w_in, w_o, attn_sinks):
    gp = ATT_GROUP // PAIRS
    wq = w_in[:, :, OFF_Q:OFF_K].reshape(DEPTH, D_MODEL, ATT_KV_HEADS, PAIRS, gp, 2, HALF_DIM)
    wq = jnp.transpose(wq, (0, 1, 3, 5, 4, 2, 6)).reshape(DEPTH, D_MODEL, ATT_WIDTH)
    wk = w_in[:, :, OFF_K:OFF_V].reshape(DEPTH, D_MODEL, ATT_KV_HEADS, 2, HALF_DIM)
    wk = jnp.transpose(wk, (0, 1, 3, 2, 4)).reshape(DEPTH, D_MODEL, ATT_KV_HEADS * HEAD_DIM)
    w_in_p = jnp.concatenate([w_in[:, :, :OFF_Q], wq, wk, w_in[:, :, OFF_V:]], axis=2)
    c0 = GM_WIDTH + POOL_WIDTH
    wo_att = w_o[:, c0:, :].reshape(DEPTH, ATT_KV_HEADS, PAIRS, gp, HEAD_DIM, D_MODEL)
    wo_att = jnp.transpose(wo_att, (0, 2, 3, 1, 4, 5)).reshape(DEPTH, ATT_WIDTH, D_MODEL)
    w_o_p = jnp.concatenate([w_o[:, :c0, :], wo_att], axis=1)
    sinks_p = jnp.transpose(attn_sinks.reshape(DEPTH, ATT_KV_HEADS, PAIRS, gp), (0, 2, 3, 1)).reshape(DEPTH, ATT_Q_HEADS)
    return w_in_p, w_o_p, sinks_p


def kernel(x, mem, positions, mem_norm_g, mix_pre_g, mix_post_g, w_in, gm_v_g, gm_w_s, gm_b_s, pool_w, pool_scale,
           attn_sinks, w_o, x_pre_g, x_post_g, w_xq, w_xkv, w_xo, ffn_pre_g, ffn_post_g, w_gate_up, w_down):
    assert x.shape[1] % TS_MIX == 0 and x.shape[1] % TM_XATTN == 0 and x.shape[1] % TM_FFN == 0
    row = lambda g: g.reshape(DEPTH, 1, -1)
    w_in_p, w_o_p, sinks_p = _attention_layouts(w_in, w_o, attn_sinks)
    w_in_b, w_o_b = w_in_p.astype(BF16), w_o_p.astype(BF16)
    w_xq_b, w_xkv_b, w_xo_b = w_xq.astype(BF16), w_xkv.astype(BF16), w_xo.astype(BF16)
    w_gu_b, w_d_b = w_gate_up.astype(BF16), w_down.astype(BF16)
    ws_cat = jnp.transpose(gm_w_s, (0, 2, 1, 3)).reshape(DEPTH, CHUNK, GM_GROUPS * CHUNK)
    bs_tile = jnp.repeat(jnp.transpose(gm_b_s, (0, 2, 1)), HEAD_DIM, axis=2)
    n_pg = len(POOL_WINDOWS)
    pw_bd = jnp.einsum('lgcd,gh->lgchd', pool_w, jnp.eye(n_pg, dtype=pool_w.dtype))
    pw_bd = pw_bd.reshape(DEPTH, POOL_WIDTH, POOL_WIDTH).astype(BF16)
    ones_bd = jnp.kron(jnp.eye(GM_GROUPS, dtype=F32), jnp.ones((HEAD_DIM, HEAD_DIM), F32)).astype(BF16)

    cos_t, sin_t = _rope_tables(positions)
    kv = _mem_kv(mem, mem_norm_g, w_xkv_b)
    for l in range(DEPTH):
        x = _mix_layer(l, x, cos_t, sin_t, sinks_p, row(mix_pre_g), w_in_b, row(gm_v_g), ones_bd, ws_cat, bs_tile,
                       pw_bd, row(pool_scale), w_o_b, row(mix_post_g))
        x = _xattn_layer(l, x, row(x_pre_g), w_xq_b, kv, w_xo_b, row(x_post_g))
        x = _ffn_layer(l, x, row(ffn_pre_g), w_gu_b, w_d_b, row(ffn_post_g))
    return x
```
